```python
import jax, jax.numpy as jnp
from jax import lax
import numpy as np

D_MODEL = 2048
BATCH = 32
SEQ = 256
DEPTH = 2
DEC_BATCH = 8
DEC_SEQ = 4096
PAST_LEN = 256

GRID_W = 64
N_MIXERS = 2
HEAD_DIM = 128
N_HEADS = D_MODEL // HEAD_DIM
N_KV_HEADS = 4
GQA_GROUP = N_HEADS // N_KV_HEADS
D_ATTN = N_HEADS * HEAD_DIM
D_KV = N_KV_HEADS * HEAD_DIM
QKV_DIM = D_ATTN + 2 * D_KV
ROPE_THETA = 10000.0
ROPE_AXIS_DIM = HEAD_DIM // 2
Q_BLOCK = 128
POOL_WINDOWS = (2, 4, 8, 16)
N_POOL_GROUPS = 4
POOL_GC = D_MODEL // N_POOL_GROUPS
D_FF = 5632
N_ATTN_LAYERS = (DEPTH + 1) // 2
N_POOL_LAYERS = DEPTH // 2
ALPHA = (2.0 * DEPTH) ** 0.25
BETA = (8.0 * DEPTH) ** -0.25
LN_EPS = 1e-5
QK_EPS = 1e-6
MOD_STD = 0.5

kernel_name = 'hybrid_flow_attn_pool_macaron_step'


def layer_norm(x, g, b):
    xf = x.astype(jnp.float32)
    mu = jnp.mean(xf, axis=-1, keepdims=True)
    var = jnp.mean(jnp.square(xf - mu), axis=-1, keepdims=True)
    y = (xf - mu) * lax.rsqrt(var + LN_EPS) * g.astype(jnp.float32) + b.astype(jnp.float32)
    return y.astype(x.dtype)


def rms_norm(x, g):
    xf = x.astype(jnp.float32)
    y = xf * lax.rsqrt(jnp.mean(jnp.square(xf), axis=-1, keepdims=True) + QK_EPS) * g.astype(jnp.float32)
    return y.astype(x.dtype)


def modulation(cond, w_mod, b_mod):
    m = jax.nn.silu(cond) @ w_mod + b_mod
    return m.reshape(cond.shape[0], 3, 3, D_MODEL)


def modulate(x, shift, scale):
    return x * (1.0 + scale[:, None, :]) + shift[:, None, :]


def post_norm_residual(x, y, gate, g, b, weight):
    return layer_norm(ALPHA * x + weight * gate[:, None, :] * y, g, b)


def ffn_sublayer(x, m, w1, w3, w2, g, b):
    h = modulate(x, m[:, 0], m[:, 1])
    f = (jax.nn.silu(h @ w1) * (h @ w3)) @ w2
    return post_norm_residual(x, f, m[:, 2], g, b, 0.5)


def axial_rope_tables(n_tokens):
    rows = n_tokens // GRID_W
    r, cgrid = jnp.meshgrid(jnp.arange(rows), jnp.arange(GRID_W), indexing='ij')
    row = r.reshape(-1).astype(jnp.float32)
    col = cgrid.reshape(-1).astype(jnp.float32)
    half = ROPE_AXIS_DIM // 2
    freqs = ROPE_THETA ** (-jnp.arange(half, dtype=jnp.float32) / half)
    ang_r = row[:, None] * freqs
    ang_c = col[:, None] * freqs
    return jnp.cos(ang_r), jnp.sin(ang_r), jnp.cos(ang_c), jnp.sin(ang_c)


def rope_half(x, cos, sin):
    x1, x2 = jnp.split(x, 2, axis=-1)
    cos = cos[:, None, :]
    sin = sin[:, None, :]
    return jnp.concatenate([x1 * cos - x2 * sin, x1 * sin + x2 * cos], axis=-1)


def apply_axial_rope(x, tables):
    cr, sr, cc, sc = tables
    xf = x.astype(jnp.float32)
    xr, xc = jnp.split(xf, 2, axis=-1)
    out = jnp.concatenate([rope_half(xr, cr, sr), rope_half(xc, cc, sc)], axis=-1)
    return out.astype(x.dtype)


def qkv_heads(h, w_qkv, q_gain, k_gain):
    bsz, n, _ = h.shape
    qkv = h @ w_qkv
    q = qkv[..., :D_ATTN].reshape(bsz, n, N_HEADS, HEAD_DIM)
    k = qkv[..., D_ATTN:D_ATTN + D_KV].reshape(bsz, n, N_KV_HEADS, HEAD_DIM)
    v = qkv[..., D_ATTN + D_KV:].reshape(bsz, n, N_KV_HEADS, HEAD_DIM)
    return rms_norm(q, q_gain), rms_norm(k, k_gain), v


def block_attention(q, k, v):
    bsz, n = q.shape[0], q.shape[1]
    nb = n // Q_BLOCK
    qb = q.reshape(bsz, nb, Q_BLOCK, N_KV_HEADS, GQA_GROUP, HEAD_DIM)
    qb = jnp.moveaxis(qb, 1, 0)
    scale = HEAD_DIM ** -0.5

    def one_block(qblk):
        s = jnp.einsum('bqkgd,blkd->bkgql', qblk, k, preferred_element_type=jnp.float32) * scale
        p = jax.nn.softmax(s, axis=-1).astype(v.dtype)
        return jnp.einsum('bkgql,blkd->bqkgd', p, v)

    o = lax.map(one_block, qb)
    return jnp.moveaxis(o, 0, 1).reshape(bsz, n, D_ATTN)


def pool_mix(h, w_pool, pool_scale):
    bsz, n, _ = h.shape
    hf = h.astype(jnp.float32)
    csum = jnp.concatenate([jnp.zeros((bsz, 1, D_MODEL), jnp.float32), jnp.cumsum(hf, axis=1)], axis=1)
    t = jnp.arange(n)
    outs = []
    for gi, win in enumerate(POOL_WINDOWS):
        lo = jnp.clip(t - win // 2, 0, n)
        hi = jnp.clip(t + win // 2, 0, n)
        cg = csum[..., gi * POOL_GC:(gi + 1) * POOL_GC]
        count = (hi - lo).astype(jnp.float32)[None, :, None]
        mean = (cg[:, hi] - cg[:, lo]) / count
        outs.append(mean - hf[..., gi * POOL_GC:(gi + 1) * POOL_GC])
    d = jnp.stack(outs, axis=2).astype(h.dtype)
    y = jnp.einsum('bngc,gcd->bngd', d, w_pool).reshape(bsz, n, D_MODEL)
    return y * pool_scale


def setup_inputs(seed: int = 0) -> dict:
    key = jax.random.key(seed)
    ks = jax.random.split(key, 20)
    f32 = jnp.float32

    def nrm(k, shape, std):
        return jax.random.normal(k, shape, f32) * std

    return {
        'x_prompt': nrm(ks[0], (BATCH, SEQ, D_MODEL), 1.0),
        'x_sample': nrm(ks[1], (DEC_BATCH, DEC_SEQ, D_MODEL), 1.0),
        'cache_k': nrm(ks[2], (DEC_BATCH, N_ATTN_LAYERS, PAST_LEN, N_KV_HEADS, HEAD_DIM), 1.0),
        'cache_v': nrm(ks[3], (DEC_BATCH, N_ATTN_LAYERS, PAST_LEN, N_KV_HEADS, HEAD_DIM), 1.0),
        'c': nrm(ks[4], (DEC_BATCH, D_MODEL), 1.0),
        'c_ctx': nrm(ks[5], (D_MODEL,), 1.0),
        'w_mod': nrm(ks[6], (DEPTH, D_MODEL, 9 * D_MODEL), MOD_STD * D_MODEL ** -0.5),
        'b_mod': nrm(ks[7], (DEPTH, 9 * D_MODEL), 0.02),
        'ln_g': 1.0 + nrm(ks[8], (DEPTH, 3, D_MODEL), 0.02),
        'ln_b': nrm(ks[9], (DEPTH, 3, D_MODEL), 0.02),
        'ffn_w1': nrm(ks[10], (DEPTH, 2, D_MODEL, D_FF), D_MODEL ** -0.5),
        'ffn_w3': nrm(ks[11], (DEPTH, 2, D_MODEL, D_FF), D_MODEL ** -0.5),
        'ffn_w2': nrm(ks[12], (DEPTH, 2, D_FF, D_MODEL), BETA * D_FF ** -0.5),
        'w_qkv': nrm(ks[13], (N_ATTN_LAYERS, D_MODEL, QKV_DIM), D_MODEL ** -0.5),
        'q_gain': 1.0 + nrm(ks[14], (N_ATTN_LAYERS, HEAD_DIM), 0.02),
        'k_gain': 1.0 + nrm(ks[15], (N_ATTN_LAYERS, HEAD_DIM), 0.02),
        'w_o': nrm(ks[16], (N_ATTN_LAYERS, D_ATTN, D_MODEL), BETA * D_ATTN ** -0.5),
        'w_pool': nrm(ks[17], (N_POOL_LAYERS, N_POOL_GROUPS, POOL_GC, POOL_GC), BETA * POOL_GC ** -0.5),
        'pool_scale': 1.0 + nrm(ks[18], (N_POOL_LAYERS, D_MODEL), 0.1),
    }


def reference(x_prompt, x_sample, cache_k, cache_v, c, c_ctx, w_mod, b_mod, ln_g, ln_b,
              ffn_w1, ffn_w3, ffn_w2, w_qkv, q_gain, k_gain, w_o, w_pool, pool_scale):
    x = x_prompt
    new_k = []
    new_v = []
    for i in range(DEPTH):
        m = modulation(c_ctx[None, :], w_mod[i], b_mod[i])
        x = ffn_sublayer(x, m[:, 0], ffn_w1[i, 0], ffn_w3[i, 0], ffn_w2[i, 0], ln_g[i, 0], ln_b[i, 0])
        h = modulate(x, m[:, 1, 0], m[:, 1, 1])
        j = i // N_MIXERS
        if i % N_MIXERS == 0:
            q, k, v = qkv_heads(h, w_qkv[j], q_gain[j], k_gain[j])
            y = block_attention(q, k, v) @ w_o[j]
            new_k.append(k)
            new_v.append(v)
        else:
            y = pool_mix(h, w_pool[j], pool_scale[j])
        x = post_norm_residual(x, y, m[:, 1, 2], ln_g[i, 1], ln_b[i, 1], 1.0)
        x = ffn_sublayer(x, m[:, 2], ffn_w1[i, 1], ffn_w3[i, 1], ffn_w2[i, 1], ln_g[i, 2], ln_b[i, 2])
    y_prompt = x
    ctx_k = jnp.stack(new_k, axis=1)
    ctx_v = jnp.stack(new_v, axis=1)

    rope = axial_rope_tables(x_sample.shape[1])
    x = x_sample
    for i in range(DEPTH):
        m = modulation(c, w_mod[i], b_mod[i])
        x = ffn_sublayer(x, m[:, 0], ffn_w1[i, 0], ffn_w3[i, 0], ffn_w2[i, 0], ln_g[i, 0], ln_b[i, 0])
        h = modulate(x, m[:, 1, 0], m[:, 1, 1])
        j = i // N_MIXERS
        if i % N_MIXERS == 0:
            q, k, v = qkv_heads(h, w_qkv[j], q_gain[j], k_gain[j])
            q = apply_axial_rope(q, rope)
            k = apply_axial_rope(k, rope)
            k_all = jnp.concatenate([cache_k[:, j].astype(k.dtype), k], axis=1)
            v_all = jnp.concatenate([cache_v[:, j].astype(v.dtype), v], axis=1)
            y = block_attention(q, k_all, v_all) @ w_o[j]
        else:
            y = pool_mix(h, w_pool[j], pool_scale[j])
        x = post_norm_residual(x, y, m[:, 1, 2], ln_g[i, 1], ln_b[i, 1], 1.0)
        x = ffn_sublayer(x, m[:, 2], ffn_w1[i, 1], ffn_w3[i, 1], ffn_w2[i, 1], ln_g[i, 2], ln_b[i, 2])
    y_sample = x
    return (y_prompt, y_sample, ctx_k, ctx_v)
```

```python
import functools

import jax
import jax.numpy as jnp
import numpy as np
from jax.experimental import pallas as pl
from jax.experimental.pallas import tpu as pltpu

D_MODEL = 2048
HEAD_DIM = 128
N_HEADS = 16
N_KV_HEADS = 4
GQA_GROUP = N_HEADS // N_KV_HEADS
D_KV = N_KV_HEADS * HEAD_DIM
QKV_DIM = D_MODEL + 2 * D_KV
D_FF = 5632
GRID_W = 64
ROPE_THETA = 10000.0
POOL_WINDOWS = (2, 4, 8, 16)
POOL_GC = D_MODEL // len(POOL_WINDOWS)
DEPTH = 2
ALPHA = (2.0 * DEPTH) ** 0.25
LN_EPS = 1e-5
QK_EPS = 1e-6
ATTN_SCALE = HEAD_DIM ** -0.5

N_COND_ROWS = 16
CTX_ROW = 8
POOL_HALO = 8

V7X_VMEM_LIMIT_BYTES = 56 * 1024 * 1024

TOKEN_TILE = 512
FF_TILE = 512
MOD_TILE = 1024
ATTN_Q_TILE = 512

BF16 = jnp.bfloat16
F32 = jnp.float32


def _compiler_params(semantics):
    return pltpu.CompilerParams(dimension_semantics=semantics, vmem_limit_bytes=V7X_VMEM_LIMIT_BYTES)


def _resident(block_shape, index_map):
    return pl.BlockSpec(block_shape, index_map, pipeline_mode=pl.Buffered(1))


def _layer_norm_rows(y, g, b):
    mu = jnp.mean(y, axis=-1, keepdims=True)
    d = y - mu
    var = jnp.mean(d * d, axis=-1, keepdims=True)
    return d * jax.lax.rsqrt(var + LN_EPS) * g + b


def _modulated_bf16(x, m_ref):
    return (x * (1.0 + m_ref[1:2, :]) + m_ref[0:1, :]).astype(BF16)


def _mod_kernel(c_ref, w_ref, b_ref, o_ref):
    c = c_ref[...]
    s = (c * jax.nn.sigmoid(c)).astype(BF16)
    o_ref[...] = jnp.dot(s, w_ref[...].astype(BF16), preferred_element_type=F32) + b_ref[...]


def _modulation_table(cond, w_mod, b_mod):
    n_out = 9 * D_MODEL
    out = pl.pallas_call(
        _mod_kernel,
        grid=(DEPTH, n_out // MOD_TILE),
        in_specs=[
            pl.BlockSpec((N_COND_ROWS, D_MODEL), lambda l, n: (0, 0)),
            pl.BlockSpec((None, D_MODEL, MOD_TILE), lambda l, n: (l, 0, n)),
            pl.BlockSpec((None, 1, MOD_TILE), lambda l, n: (l, 0, n)),
        ],
        out_specs=pl.BlockSpec((None, N_COND_ROWS, MOD_TILE), lambda l, n: (l, 0, n)),
        out_shape=jax.ShapeDtypeStruct((DEPTH, N_COND_ROWS, n_out), F32),
        compiler_params=_compiler_params(("arbitrary", "arbitrary")),
        name="modulation",
    )(cond, w_mod, b_mod.reshape(DEPTH, 1, n_out))
    return out.reshape(DEPTH * N_COND_ROWS * 3, 3, D_MODEL)


def _mod_spec(layer, sub, cond_len, tile=TOKEN_TILE):
    def index_map(i, *_):
        row = CTX_ROW if cond_len is None else (i * tile) // cond_len
        return ((layer * N_COND_ROWS + row) * 3 + sub, 0, 0)
    return pl.BlockSpec((None, 3, D_MODEL), index_map)


def _ffn_kernel(x_ref, m_ref, w1_ref, w3_ref, w2_ref, g_ref, b_ref, o_ref, h_ref, acc_ref):
    j = pl.program_id(1)

    @pl.when(j == 0)
    def _():
        h_ref[...] = _modulated_bf16(x_ref[...], m_ref)
        acc_ref[...] = jnp.zeros_like(acc_ref)

    h = h_ref[...]
    a = jnp.dot(h, w1_ref[...], preferred_element_type=F32)
    b = jnp.dot(h, w3_ref[...], preferred_element_type=F32)
    u = (a * jax.nn.sigmoid(a) * b).astype(BF16)
    acc_ref[...] += jnp.dot(u, w2_ref[...], preferred_element_type=F32)

    @pl.when(j == pl.num_programs(1) - 1)
    def _():
        y = ALPHA * x_ref[...] + (0.5 * m_ref[2:3, :]) * acc_ref[...]
        o_ref[...] = _layer_norm_rows(y, g_ref[...], b_ref[...])


def _ffn_sublayer(x, mods, layer, sub, cond_len, w1, w3, w2, g, b):
    n_tok = x.shape[0]
    return pl.pallas_call(
        _ffn_kernel,
        grid=(n_tok // TOKEN_TILE, D_FF // FF_TILE),
        in_specs=[
            pl.BlockSpec((TOKEN_TILE, D_MODEL), lambda i, j: (i, 0)),
            _mod_spec(layer, sub, cond_len),
            pl.BlockSpec((D_MODEL, FF_TILE), lambda i, j: (0, j)),
            pl.BlockSpec((D_MODEL, FF_TILE), lambda i, j: (0, j)),
            pl.BlockSpec((FF_TILE, D_MODEL), lambda i, j: (j, 0)),
            pl.BlockSpec((1, D_MODEL), lambda i, j: (0, 0)),
            pl.BlockSpec((1, D_MODEL), lambda i, j: (0, 0)),
        ],
        out_specs=pl.BlockSpec((TOKEN_TILE, D_MODEL), lambda i, j: (i, 0)),
        out_shape=jax.ShapeDtypeStruct((n_tok, D_MODEL), F32),
        scratch_shapes=[pltpu.VMEM((TOKEN_TILE, D_MODEL), BF16), pltpu.VMEM((TOKEN_TILE, D_MODEL), F32)],
        compiler_params=_compiler_params(("arbitrary", "arbitrary")),
        name="ffn",
    )(x, mods, w1, w3, w2, g.reshape(1, D_MODEL), b.reshape(1, D_MODEL))


def _swap_32_lane_halves(x):
    lane = jax.lax.broadcasted_iota(jnp.int32, x.shape, 1)
    up = pltpu.roll(x, HEAD_DIM - 32, 1)
    down = pltpu.roll(x, 32, 1)
    return jnp.where((lane & 32) == 0, up, down)


def _qkv_kernel(*refs, rope):
    if rope:
        x_ref, m_ref, w_ref, qg_ref, kg_ref, cos_ref, sin_ref, q_ref, k_ref, v_ref = refs
    else:
        x_ref, m_ref, w_ref, qg_ref, kg_ref, q_ref, k_ref, v_ref = refs
    h = _modulated_bf16(x_ref[...], m_ref)
    qkv = jnp.dot(h, w_ref[...], preferred_element_type=F32)

    def normed_head(col, gain):
        t = qkv[:, col:col + HEAD_DIM]
        t = t * jax.lax.rsqrt(jnp.mean(t * t, axis=-1, keepdims=True) + QK_EPS) * gain
        if rope:
            t = t * cos_ref[...] + _swap_32_lane_halves(t) * sin_ref[...]
        return t

    for hd in range(N_HEADS):
        c = hd * HEAD_DIM
        q_ref[:, c:c + HEAD_DIM] = (normed_head(c, qg_ref[...]) * ATTN_SCALE).astype(q_ref.dtype)
    for hd in range(N_KV_HEADS):
        c = hd * HEAD_DIM
        k_ref[:, c:c + HEAD_DIM] = normed_head(D_MODEL + c, kg_ref[...]).astype(k_ref.dtype)
    v_ref[...] = qkv[:, D_MODEL + D_KV:].astype(v_ref.dtype)


def _qkv_heads(x, mods, layer, cond_len, seq_len, w_qkv, q_gain, k_gain, rope_tables, kv_dtype):
    n_tok = x.shape[0]
    rope = rope_tables is not None
    tok = lambda i: (i, 0)
    const = lambda i: (0, 0)
    in_specs = [
        pl.BlockSpec((TOKEN_TILE, D_MODEL), tok),
        _mod_spec(layer, 1, cond_len),
        _resident((D_MODEL, QKV_DIM), const),
        pl.BlockSpec((1, HEAD_DIM), const),
        pl.BlockSpec((1, HEAD_DIM), const),
    ]
    args = [x, mods, w_qkv, q_gain.reshape(1, HEAD_DIM), k_gain.reshape(1, HEAD_DIM)]
    if rope:
        tiles_per_seq = seq_len // TOKEN_TILE
        pos = lambda i: (i % tiles_per_seq, 0)
        in_specs += [pl.BlockSpec((TOKEN_TILE, HEAD_DIM), pos), pl.BlockSpec((TOKEN_TILE, HEAD_DIM), pos)]
        args += list(rope_tables)
    return pl.pallas_call(
        functools.partial(_qkv_kernel, rope=rope),
        grid=(n_tok // TOKEN_TILE,),
        in_specs=in_specs,
        out_specs=[
            pl.BlockSpec((TOKEN_TILE, D_MODEL), tok),
            pl.BlockSpec((TOKEN_TILE, D_KV), tok),
            pl.BlockSpec((TOKEN_TILE, D_KV), tok),
        ],
        out_shape=[
            jax.ShapeDtypeStruct((n_tok, D_MODEL), BF16),
            jax.ShapeDtypeStruct((n_tok, D_KV), kv_dtype),
            jax.ShapeDtypeStruct((n_tok, D_KV), kv_dtype),
        ],
        compiler_params=_compiler_params(("arbitrary",)),
        name="qkv_rope" if rope else "qkv",
    )(*args)


def _rope_tables(n_tokens):
    pos = np.arange(n_tokens)
    half = HEAD_DIM // 4
    freqs = jnp.asarray(ROPE_THETA, F32) ** (-jnp.arange(half, dtype=F32) / half)
    ang_r = jnp.asarray(pos // GRID_W, F32)[:, None] * freqs
    ang_c = jnp.asarray(pos % GRID_W, F32)[:, None] * freqs
    cos = jnp.concatenate([jnp.cos(ang_r)] * 2 + [jnp.cos(ang_c)] * 2, axis=-1)
    sin = jnp.concatenate([-jnp.sin(ang_r), jnp.sin(ang_r), -jnp.sin(ang_c), jnp.sin(ang_c)], axis=-1)
    return cos, sin


def _softmax_pv(s_list, v_list):
    m = s_list[0].max(axis=-1, keepdims=True)
    for s in s_list[1:]:
        m = jnp.maximum(m, s.max(axis=-1, keepdims=True))
    l = None
    o = None
    for s, v in zip(s_list, v_list):
        p = jnp.exp(s - m)
        ls = p.sum(axis=-1, keepdims=True)
        os_ = jnp.dot(p.astype(BF16), v, preferred_element_type=F32)
        l = ls if l is None else l + ls
        o = os_ if o is None else o + os_
    return o / l


def _qk(q, k):
    return jax.lax.dot_general(q, k, (((1,), (1,)), ((), ())), preferred_element_type=F32)


def _latent_attn_kernel(q_ref, kc_ref, vc_ref, k_ref, v_ref, o_ref):
    q = q_ref[...]
    kc = kc_ref[...].astype(BF16)
    vc = vc_ref[...].astype(BF16)
    o = _softmax_pv([_qk(q, kc), _qk(q, k_ref[...])], [vc, v_ref[...]])
    o_ref[...] = o.astype(o_ref.dtype)


def _latent_attention(q, k, v, cache_k, cache_v, n_batch, seq_len):
    past = cache_k.shape[1]
    q_tiles = seq_len // ATTN_Q_TILE
    return pl.pallas_call(
        _latent_attn_kernel,
        grid=(n_batch, N_HEADS, q_tiles),
        in_specs=[
            pl.BlockSpec((ATTN_Q_TILE, HEAD_DIM), lambda b, h, t: (b * q_tiles + t, h)),
            pl.BlockSpec((None, past, HEAD_DIM), lambda b, h, t: (b, 0, h // GQA_GROUP)),
            pl.BlockSpec((None, past, HEAD_DIM), lambda b, h, t: (b, 0, h // GQA_GROUP)),
            pl.BlockSpec((seq_len, HEAD_DIM), lambda b, h, t: (b, h // GQA_GROUP)),
            pl.BlockSpec((seq_len, HEAD_DIM), lambda b, h, t: (b, h // GQA_GROUP)),
        ],
        out_specs=pl.BlockSpec((ATTN_Q_TILE, HEAD_DIM), lambda b, h, t: (b * q_tiles + t, h)),
        out_shape=jax.ShapeDtypeStruct(q.shape, BF16),
        compiler_params=_compiler_params(("arbitrary", "arbitrary", "arbitrary")),
        name="latent_attention",
    )(q, cache_k, cache_v, k, v)


def _context_attn_kernel(q_ref, k_ref, v_ref, o_ref):
    for g in range(N_KV_HEADS):
        k = k_ref[:, g * HEAD_DIM:(g + 1) * HEAD_DIM].astype(BF16)
        v = v_ref[:, g * HEAD_DIM:(g + 1) * HEAD_DIM].astype(BF16)
        for jh in range(GQA_GROUP):
            c = (g * GQA_GROUP + jh) * HEAD_DIM
            o = _softmax_pv([_qk(q_ref[:, c:c + HEAD_DIM], k)], [v])
            o_ref[:, c:c + HEAD_DIM] = o.astype(o_ref.dtype)


def _context_attention(q, k, v, n_batch, seq_len):
    return pl.pallas_call(
        _context_attn_kernel,
        grid=(n_batch,),
        in_specs=[
            pl.BlockSpec((seq_len, D_MODEL), lambda b: (b, 0)),
            pl.BlockSpec((seq_len, D_KV), lambda b: (b, 0)),
            pl.BlockSpec((seq_len, D_KV), lambda b: (b, 0)),
        ],
        out_specs=pl.BlockSpec((seq_len, D_MODEL), lambda b: (b, 0)),
        out_shape=jax.ShapeDtypeStruct(q.shape, BF16),
        compiler_params=_compiler_params(("arbitrary",)),
        name="context_attention",
    )(q, k, v)


def _out_proj_kernel(a_ref, x_ref, m_ref, w_ref, g_ref, b_ref, o_ref):
    y = jnp.dot(a_ref[...], w_ref[...], preferred_element_type=F32)
    z = ALPHA * x_ref[...] + m_ref[2:3, :] * y
    o_ref[...] = _layer_norm_rows(z, g_ref[...], b_ref[...])


def _out_proj_sublayer(attn, x, mods, layer, cond_len, w_o, g, b):
    n_tok = x.shape[0]
    tok = lambda i: (i, 0)
    const = lambda i: (0, 0)
    return pl.pallas_call(
        _out_proj_kernel,
        grid=(n_tok // TOKEN_TILE,),
        in_specs=[
            pl.BlockSpec((TOKEN_TILE, D_MODEL), tok),
            pl.BlockSpec((TOKEN_TILE, D_MODEL), tok),
            _mod_spec(layer, 1, cond_len),
            _resident((D_MODEL, D_MODEL), const),
            pl.BlockSpec((1, D_MODEL), const),
            pl.BlockSpec((1, D_MODEL), const),
        ],
        out_specs=pl.BlockSpec((TOKEN_TILE, D_MODEL), tok),
        out_shape=jax.ShapeDtypeStruct((n_tok, D_MODEL), F32),
        compiler_params=_compiler_params(("arbitrary",)),
        name="out_proj",
    )(attn, x, mods, w_o, g.reshape(1, D_MODEL), b.reshape(1, D_MODEL))


def _pool_kernel(x_ref, prev_ref, next_ref, m_ref, w_ref, ps_ref, g_ref, b_ref, o_ref, h_ref,
                 *, tile, tiles_per_seq):
    i = pl.program_id(0)
    t_in_seq = i % tiles_per_seq
    scale1 = 1.0 + m_ref[1:2, :]
    shift = m_ref[0:1, :]
    x = x_ref[...]
    h_ref[POOL_HALO:POOL_HALO + tile, :] = x * scale1 + shift
    h_ref[0:POOL_HALO, :] = jnp.where(t_in_seq > 0, prev_ref[...] * scale1 + shift, 0.0)
    h_ref[POOL_HALO + tile:, :] = jnp.where(t_in_seq < tiles_per_seq - 1, next_ref[...] * scale1 + shift, 0.0)

    seq_len = tile * tiles_per_seq
    pos = t_in_seq * tile + jax.lax.broadcasted_iota(jnp.int32, (tile, 1), 0)
    ys = []
    for gi, win in enumerate(POOL_WINDOWS):
        cols = slice(gi * POOL_GC, (gi + 1) * POOL_GC)
        acc = None
        for k in range(-(win // 2), win // 2):
            term = h_ref[POOL_HALO + k:POOL_HALO + k + tile, cols]
            acc = term if acc is None else acc + term
        count = jnp.minimum(pos + win // 2, seq_len) - jnp.maximum(pos - win // 2, 0)
        d = acc / count.astype(F32) - h_ref[POOL_HALO:POOL_HALO + tile, cols]
        ys.append(jnp.dot(d.astype(BF16), w_ref[gi], preferred_element_type=F32))
    y = jnp.concatenate(ys, axis=-1) * ps_ref[...]
    z = ALPHA * x + m_ref[2:3, :] * y
    o_ref[...] = _layer_norm_rows(z, g_ref[...], b_ref[...])


def _pool_sublayer(x, mods, layer, cond_len, seq_len, w_pool, pool_scale, g, b):
    n_tok = x.shape[0]
    tile = min(TOKEN_TILE, seq_len)
    tiles_per_seq = seq_len // tile
    halo_blocks = tile // POOL_HALO
    n_halo_blocks = n_tok // POOL_HALO
    tok = lambda i: (i, 0)
    const = lambda i: (0, 0)
    return pl.pallas_call(
        functools.partial(_pool_kernel, tile=tile, tiles_per_seq=tiles_per_seq),
        grid=(n_tok // tile,),
        in_specs=[
            pl.BlockSpec((tile, D_MODEL), tok),
            pl.BlockSpec((POOL_HALO, D_MODEL), lambda i: (jnp.maximum(i * halo_blocks - 1, 0), 0)),
            pl.BlockSpec((POOL_HALO, D_MODEL),
                         lambda i: (jnp.minimum((i + 1) * halo_blocks, n_halo_blocks - 1), 0)),
            _mod_spec(layer, 1, cond_len, tile),
            _resident((len(POOL_WINDOWS), POOL_GC, POOL_GC), lambda i: (0, 0, 0)),
            pl.BlockSpec((1, D_MODEL), const),
            pl.BlockSpec((1, D_MODEL), const),
            pl.BlockSpec((1, D_MODEL), const),
        ],
        out_specs=pl.BlockSpec((tile, D_MODEL), tok),
        out_shape=jax.ShapeDtypeStruct((n_tok, D_MODEL), F32),
        scratch_shapes=[pltpu.VMEM((tile + 2 * POOL_HALO, D_MODEL), F32)],
        compiler_params=_compiler_params(("arbitrary",)),
        name="pool",
    )(x, x, x, mods, w_pool, pool_scale.reshape(1, D_MODEL), g.reshape(1, D_MODEL), b.reshape(1, D_MODEL))


def kernel(x_prompt, x_sample, cache_k, cache_v, c, c_ctx, w_mod, b_mod, ln_g, ln_b,
           ffn_w1, ffn_w3, ffn_w2, w_qkv, q_gain, k_gain, w_o, w_pool, pool_scale):
    n_ctx, ctx_len, _ = x_prompt.shape
    n_lat, lat_len, _ = x_sample.shape
    past = cache_k.shape[2]

    cond = jnp.zeros((N_COND_ROWS, D_MODEL), F32).at[:n_lat].set(c).at[CTX_ROW].set(c_ctx)
    mods = _modulation_table(cond, w_mod, b_mod)

    w1, w3, w2 = ffn_w1.astype(BF16), ffn_w3.astype(BF16), ffn_w2.astype(BF16)
    wqkv, wo, wp = w_qkv.astype(BF16), w_o.astype(BF16), w_pool.astype(BF16)
    rope = _rope_tables(lat_len)

    def backbone(x, cond_len, seq_len, latent):
        new_k, new_v = [], []
        for i in range(DEPTH):
            x = _ffn_sublayer(x, mods, i, 0, cond_len, w1[i, 0], w3[i, 0], w2[i, 0], ln_g[i, 0], ln_b[i, 0])
            j = i // 2
            if i % 2 == 0:
                if latent:
                    q, k, v = _qkv_heads(x, mods, i, cond_len, seq_len, wqkv[j], q_gain[j], k_gain[j], rope, BF16)
                    ck = cache_k[:, j].reshape(n_lat, past, D_KV)
                    cv = cache_v[:, j].reshape(n_lat, past, D_KV)
                    attn = _latent_attention(q, k, v, ck, cv, n_lat, seq_len)
                else:
                    q, k, v = _qkv_heads(x, mods, i, cond_len, seq_len, wqkv[j], q_gain[j], k_gain[j], None, F32)
                    attn = _context_attention(q, k, v, n_ctx, seq_len)
                    new_k.append(k.reshape(n_ctx, seq_len, N_KV_HEADS, HEAD_DIM))
                    new_v.append(v.reshape(n_ctx, seq_len, N_KV_HEADS, HEAD_DIM))
                x = _out_proj_sublayer(attn, x, mods, i, cond_len, wo[j], ln_g[i, 1], ln_b[i, 1])
            else:
                x = _pool_sublayer(x, mods, i, cond_len, seq_len, wp[j], pool_scale[j], ln_g[i, 1], ln_b[i, 1])
            x = _ffn_sublayer(x, mods, i, 2, cond_len, w1[i, 1], w3[i, 1], w2[i, 1], ln_g[i, 2], ln_b[i, 2])
        return x, new_k, new_v

    y_ctx, ctx_k, ctx_v = backbone(x_prompt.reshape(n_ctx * ctx_len, D_MODEL), None, ctx_len, False)
    y_lat, _, _ = backbone(x_sample.reshape(n_lat * lat_len, D_MODEL), lat_len, lat_len, True)
    return (y_ctx.reshape(x_prompt.shape), y_lat.reshape(x_sample.shape),
            jnp.stack(ctx_k, axis=1), jnp.stack(ctx_v, axis=1))
```

```python
import functools

import jax
import jax.numpy as jnp
import numpy as np
from jax.experimental import pallas as pl
from jax.experimental.pallas import tpu as pltpu

D_MODEL = 2048
HEAD_DIM = 128
N_HEADS = 16
N_KV_HEADS = 4
GQA_GROUP = N_HEADS // N_KV_HEADS
D_KV = N_KV_HEADS * HEAD_DIM
QKV_DIM = D_MODEL + 2 * D_KV
D_FF = 5632
GRID_W = 64
ROPE_THETA = 10000.0
POOL_WINDOWS = (2, 4, 8, 16)
POOL_GC = D_MODEL // len(POOL_WINDOWS)
DEPTH = 2
ALPHA = (2.0 * DEPTH) ** 0.25
LN_EPS = 1e-5
QK_EPS = 1e-6
ATTN_SCALE = HEAD_DIM ** -0.5
LOG2E = 1.4426950408889634
Q_SCALE = ATTN_SCALE * LOG2E

N_COND_ROWS = 16
CTX_ROW = 8
POOL_HALO = 8
VALUE_WIDTH = 256

V7X_VMEM_LIMIT_BYTES = 56 * 1024 * 1024

TOKEN_TILE = 512
FF_TILE = 512
MOD_TILE = 1024
ATTN_Q_TILE = 512
OUT_PROJ_ROW_CHUNK = 128

BF16 = jnp.bfloat16
F32 = jnp.float32


def _compiler_params(semantics):
    return pltpu.CompilerParams(dimension_semantics=semantics, vmem_limit_bytes=V7X_VMEM_LIMIT_BYTES)


def _resident(block_shape, index_map):
    return pl.BlockSpec(block_shape, index_map, pipeline_mode=pl.Buffered(1))


def _layer_norm_rows(y, g, b):
    mu = jnp.mean(y, axis=-1, keepdims=True)
    d = y - mu
    var = jnp.mean(d * d, axis=-1, keepdims=True)
    return d * jax.lax.rsqrt(var + LN_EPS) * g + b


def _modulated_bf16(x, m_ref):
    return (x * (1.0 + m_ref[1:2, :]) + m_ref[0:1, :]).astype(BF16)


def _mod_kernel(c_ref, w_ref, b_ref, o_ref):
    c = c_ref[...]
    s = (c * jax.nn.sigmoid(c)).astype(BF16)
    o_ref[...] = jnp.dot(s, w_ref[...].astype(BF16), preferred_element_type=F32) + b_ref[...]


def _modulation_table(cond, w_mod, b_mod):
    n_out = 9 * D_MODEL
    out = pl.pallas_call(
        _mod_kernel,
        grid=(DEPTH, n_out // MOD_TILE),
        in_specs=[
            pl.BlockSpec((N_COND_ROWS, D_MODEL), lambda l, n: (0, 0)),
            pl.BlockSpec((None, D_MODEL, MOD_TILE), lambda l, n: (l, 0, n)),
            pl.BlockSpec((None, 1, MOD_TILE), lambda l, n: (l, 0, n)),
        ],
        out_specs=pl.BlockSpec((None, N_COND_ROWS, MOD_TILE), lambda l, n: (l, 0, n)),
        out_shape=jax.ShapeDtypeStruct((DEPTH, N_COND_ROWS, n_out), F32),
        compiler_params=_compiler_params(("arbitrary", "arbitrary")),
        name="modulation",
    )(cond, w_mod, b_mod.reshape(DEPTH, 1, n_out))
    return out.reshape(DEPTH * N_COND_ROWS * 3, 3, D_MODEL)


def _mod_spec(layer, sub, cond_len, tile=TOKEN_TILE):
    def index_map(i, *_):
        row = CTX_ROW if cond_len is None else (i * tile) // cond_len
        return ((layer * N_COND_ROWS + row) * 3 + sub, 0, 0)
    return pl.BlockSpec((None, 3, D_MODEL), index_map)


def _ffn_kernel(x_ref, m_ref, w1_ref, w3_ref, w2_ref, g_ref, b_ref, o_ref, h_ref, acc_ref):
    j = pl.program_id(1)

    @pl.when(j == 0)
    def _():
        h_ref[...] = _modulated_bf16(x_ref[...], m_ref)
        acc_ref[...] = jnp.zeros_like(acc_ref)

    h = h_ref[...]
    a = jnp.dot(h, w1_ref[...], preferred_element_type=F32)
    b = jnp.dot(h, w3_ref[...], preferred_element_type=F32)
    u = (a * jax.nn.sigmoid(a) * b).astype(BF16)
    acc_ref[...] += jnp.dot(u, w2_ref[...], preferred_element_type=F32)

    @pl.when(j == pl.num_programs(1) - 1)
    def _():
        y = ALPHA * x_ref[...] + (0.5 * m_ref[2:3, :]) * acc_ref[...]
        o_ref[...] = _layer_norm_rows(y, g_ref[...], b_ref[...])


def _ffn_sublayer(x, mods, layer, sub, cond_len, w1, w3, w2, g, b):
    n_tok = x.shape[0]
    return pl.pallas_call(
        _ffn_kernel,
        grid=(n_tok // TOKEN_TILE, D_FF // FF_TILE),
        in_specs=[
            pl.BlockSpec((TOKEN_TILE, D_MODEL), lambda i, j: (i, 0)),
            _mod_spec(layer, sub, cond_len),
            pl.BlockSpec((D_MODEL, FF_TILE), lambda i, j: (0, j)),
            pl.BlockSpec((D_MODEL, FF_TILE), lambda i, j: (0, j)),
            pl.BlockSpec((FF_TILE, D_MODEL), lambda i, j: (j, 0)),
            pl.BlockSpec((1, D_MODEL), lambda i, j: (0, 0)),
            pl.BlockSpec((1, D_MODEL), lambda i, j: (0, 0)),
        ],
        out_specs=pl.BlockSpec((TOKEN_TILE, D_MODEL), lambda i, j: (i, 0)),
        out_shape=jax.ShapeDtypeStruct((n_tok, D_MODEL), F32),
        scratch_shapes=[pltpu.VMEM((TOKEN_TILE, D_MODEL), BF16), pltpu.VMEM((TOKEN_TILE, D_MODEL), F32)],
        compiler_params=_compiler_params(("arbitrary", "arbitrary")),
        name="ffn",
    )(x, mods, w1, w3, w2, g.reshape(1, D_MODEL), b.reshape(1, D_MODEL))


def _swap_32_lane_halves(x):
    lane = jax.lax.broadcasted_iota(jnp.int32, x.shape, 1)
    up = pltpu.roll(x, HEAD_DIM - 32, 1)
    down = pltpu.roll(x, 32, 1)
    return jnp.where((lane & 32) == 0, up, down)


def _qkv_kernel(*refs, latent):
    if latent:
        x_ref, m_ref, w_ref, qg_ref, kg_ref, cos_ref, sin_ref, q_ref, k_ref, v_ref = refs
    else:
        x_ref, m_ref, w_ref, qg_ref, kg_ref, q_ref, k_ref, v_ref = refs
    h = _modulated_bf16(x_ref[...], m_ref)
    qkv = jnp.dot(h, w_ref[...], preferred_element_type=F32)

    def projected(col):
        return qkv[:, col:col + HEAD_DIM]

    def normed_head(col, gain):
        t = projected(col)
        t = t * jax.lax.rsqrt(jnp.mean(t * t, axis=-1, keepdims=True) + QK_EPS) * gain
        if latent:
            t = t * cos_ref[...] + _swap_32_lane_halves(t) * sin_ref[...]
        return t

    for hd in range(N_HEADS):
        c = hd * HEAD_DIM
        q_ref[:, c:c + HEAD_DIM] = (normed_head(c, qg_ref[...]) * Q_SCALE).astype(q_ref.dtype)
    if latent:
        lane = jax.lax.broadcasted_iota(jnp.int32, (x_ref.shape[0], VALUE_WIDTH - HEAD_DIM), 1)
        ones_col = jnp.where(lane == 0, 1.0, 0.0).astype(v_ref.dtype)
        for hd in range(N_KV_HEADS):
            c = hd * HEAD_DIM
            k_ref[hd] = normed_head(D_MODEL + c, kg_ref[...]).T.astype(k_ref.dtype)
            v_ref[:, hd * VALUE_WIDTH:hd * VALUE_WIDTH + HEAD_DIM] = projected(D_MODEL + D_KV + c).astype(v_ref.dtype)
            v_ref[:, hd * VALUE_WIDTH + HEAD_DIM:(hd + 1) * VALUE_WIDTH] = ones_col
    else:
        for hd in range(N_KV_HEADS):
            c = hd * HEAD_DIM
            k_ref[:, c:c + HEAD_DIM] = normed_head(D_MODEL + c, kg_ref[...])
            v_ref[:, c:c + HEAD_DIM] = projected(D_MODEL + D_KV + c)


def _qkv_heads(x, mods, layer, cond_len, seq_len, w_qkv, q_gain, k_gain, rope_tables):
    n_tok = x.shape[0]
    latent = rope_tables is not None
    tiles_per_seq = seq_len // TOKEN_TILE
    tok = lambda i: (i, 0)
    const = lambda i: (0, 0)
    in_specs = [
        pl.BlockSpec((TOKEN_TILE, D_MODEL), tok),
        _mod_spec(layer, 1, cond_len),
        _resident((D_MODEL, QKV_DIM), const),
        pl.BlockSpec((1, HEAD_DIM), const),
        pl.BlockSpec((1, HEAD_DIM), const),
    ]
    args = [x, mods, w_qkv, q_gain.reshape(1, HEAD_DIM), k_gain.reshape(1, HEAD_DIM)]
    if latent:
        pos = lambda i: (i % tiles_per_seq, 0)
        in_specs += [pl.BlockSpec((TOKEN_TILE, HEAD_DIM), pos), pl.BlockSpec((TOKEN_TILE, HEAD_DIM), pos)]
        args += list(rope_tables)
        kv_specs = [
            pl.BlockSpec((None, N_KV_HEADS, HEAD_DIM, TOKEN_TILE),
                         lambda i: (i // tiles_per_seq, 0, 0, i % tiles_per_seq)),
            pl.BlockSpec((TOKEN_TILE, N_KV_HEADS * VALUE_WIDTH), tok),
        ]
        kv_shapes = [
            jax.ShapeDtypeStruct((n_tok // seq_len, N_KV_HEADS, HEAD_DIM, seq_len), BF16),
            jax.ShapeDtypeStruct((n_tok, N_KV_HEADS * VALUE_WIDTH), BF16),
        ]
    else:
        kv_specs = [pl.BlockSpec((TOKEN_TILE, D_KV), tok)] * 2
        kv_shapes = [jax.ShapeDtypeStruct((n_tok, D_KV), F32)] * 2
    return pl.pallas_call(
        functools.partial(_qkv_kernel, latent=latent),
        grid=(n_tok // TOKEN_TILE,),
        in_specs=in_specs,
        out_specs=[pl.BlockSpec((TOKEN_TILE, D_MODEL), tok)] + kv_specs,
        out_shape=[jax.ShapeDtypeStruct((n_tok, D_MODEL), BF16)] + kv_shapes,
        compiler_params=_compiler_params(("arbitrary",)),
        name="qkv_rope" if latent else "qkv",
    )(*args)


def _rope_tables(n_tokens):
    pos = np.arange(n_tokens)
    half = HEAD_DIM // 4
    freqs = jnp.asarray(ROPE_THETA, F32) ** (-jnp.arange(half, dtype=F32) / half)
    ang_r = jnp.asarray(pos // GRID_W, F32)[:, None] * freqs
    ang_c = jnp.asarray(pos % GRID_W, F32)[:, None] * freqs
    cos = jnp.concatenate([jnp.cos(ang_r)] * 2 + [jnp.cos(ang_c)] * 2, axis=-1)
    sin = jnp.concatenate([-jnp.sin(ang_r), jnp.sin(ang_r), -jnp.sin(ang_c), jnp.sin(ang_c)], axis=-1)
    return cos, sin


def _qk_nt(q, k):
    return jax.lax.dot_general(q, k, (((1,), (1,)), ((), ())), preferred_element_type=F32)


def _with_ones_column(v):
    lane = jax.lax.broadcasted_iota(jnp.int32, (v.shape[0], VALUE_WIDTH - HEAD_DIM), 1)
    return jnp.concatenate([v.astype(BF16), jnp.where(lane == 0, 1.0, 0.0).astype(BF16)], axis=1)


def _latent_attn_kernel(q_ref, kc_ref, kt_ref, vc_ref, v_ref, o_ref, sc_ref, s_ref, m_ref):
    t = pl.program_id(0)

    @pl.when(t == 0)
    def _():
        sc_ref[...] = jnp.zeros_like(sc_ref)
        s_ref[...] = jnp.zeros_like(s_ref)
        m_ref[...] = jnp.zeros_like(m_ref)

    m = m_ref[...]
    pc = jnp.exp2(sc_ref[...] - m).astype(BF16)
    p = jnp.exp2(s_ref[...] - m).astype(BF16)
    vc = _with_ones_column(vc_ref[...])
    half = ATTN_Q_TILE // 2
    outs = []
    for r in range(2):
        rows = slice(r * half, (r + 1) * half)
        acc = (jnp.dot(pc[rows], vc, preferred_element_type=F32)
               + jnp.dot(p[rows], v_ref[...], preferred_element_type=F32))
        outs.append(acc[:, :HEAD_DIM] / acc[:, HEAD_DIM:HEAD_DIM + 1])
    o_ref[...] = jnp.concatenate(outs, axis=0).astype(o_ref.dtype)

    q = q_ref[...]
    sc = _qk_nt(q, kc_ref[...].astype(BF16))
    s = jnp.dot(q, kt_ref[...], preferred_element_type=F32)
    sc_ref[...] = sc
    s_ref[...] = s
    m_ref[...] = jnp.maximum(sc.max(axis=-1, keepdims=True), s.max(axis=-1, keepdims=True))


def _latent_attention(q, k_t, v_wide, cache_k, cache_v, n_batch, seq_len):
    past = cache_k.shape[1]
    q_tiles = seq_len // ATTN_Q_TILE
    n_steps = n_batch * N_HEADS * q_tiles

    def coords(step):
        return step // (N_HEADS * q_tiles), (step // q_tiles) % N_HEADS, step % q_tiles

    def scored(t):
        return coords(jnp.minimum(t, n_steps - 1))

    def finished(t):
        return coords(jnp.maximum(t - 1, 0))

    def q_map(t):
        b, h, i = scored(t)
        return (b * q_tiles + i, h)

    def o_map(t):
        b, h, i = finished(t)
        return (b * q_tiles + i, h)

    def kc_map(t):
        b, h, _ = scored(t)
        return (b, 0, h // GQA_GROUP)

    def kt_map(t):
        b, h, _ = scored(t)
        return (b, h // GQA_GROUP, 0, 0)

    def vc_map(t):
        b, h, _ = finished(t)
        return (b, 0, h // GQA_GROUP)

    def v_map(t):
        b, h, _ = finished(t)
        return (b, h // GQA_GROUP)

    return pl.pallas_call(
        _latent_attn_kernel,
        grid=(n_steps + 1,),
        in_specs=[
            pl.BlockSpec((ATTN_Q_TILE, HEAD_DIM), q_map),
            pl.BlockSpec((None, past, HEAD_DIM), kc_map),
            pl.BlockSpec((None, None, HEAD_DIM, seq_len), kt_map),
            pl.BlockSpec((None, past, HEAD_DIM), vc_map),
            pl.BlockSpec((seq_len, VALUE_WIDTH), v_map),
        ],
        out_specs=pl.BlockSpec((ATTN_Q_TILE, HEAD_DIM), o_map),
        out_shape=jax.ShapeDtypeStruct(q.shape, BF16),
        scratch_shapes=[
            pltpu.VMEM((ATTN_Q_TILE, past), F32),
            pltpu.VMEM((ATTN_Q_TILE, seq_len), F32),
            pltpu.VMEM((ATTN_Q_TILE, 1), F32),
        ],
        compiler_params=_compiler_params(("arbitrary",)),
        name="latent_attention",
    )(q, cache_k, k_t, cache_v, v_wide)


def _context_attn_kernel(q_ref, k_ref, v_ref, o_ref):
    for g in range(N_KV_HEADS):
        k = k_ref[:, g * HEAD_DIM:(g + 1) * HEAD_DIM].astype(BF16)
        v = v_ref[:, g * HEAD_DIM:(g + 1) * HEAD_DIM].astype(BF16)
        for jh in range(GQA_GROUP):
            c = (g * GQA_GROUP + jh) * HEAD_DIM
            s = _qk_nt(q_ref[:, c:c + HEAD_DIM], k)
            p = jnp.exp2(s - s.max(axis=-1, keepdims=True))
            o = jnp.dot(p.astype(BF16), v, preferred_element_type=F32) / p.sum(axis=-1, keepdims=True)
            o_ref[:, c:c + HEAD_DIM] = o.astype(o_ref.dtype)


def _context_attention(q, k, v, n_batch, seq_len):
    return pl.pallas_call(
        _context_attn_kernel,
        grid=(n_batch,),
        in_specs=[
            pl.BlockSpec((seq_len, D_MODEL), lambda b: (b, 0)),
            pl.BlockSpec((seq_len, D_KV), lambda b: (b, 0)),
            pl.BlockSpec((seq_len, D_KV), lambda b: (b, 0)),
        ],
        out_specs=pl.BlockSpec((seq_len, D_MODEL), lambda b: (b, 0)),
        out_shape=jax.ShapeDtypeStruct(q.shape, BF16),
        compiler_params=_compiler_params(("arbitrary",)),
        name="context_attention",
    )(q, k, v)


def _out_proj_kernel(a_ref, x_ref, m_ref, w_ref, g_ref, b_ref, o_ref):
    for r in range(0, a_ref.shape[0], OUT_PROJ_ROW_CHUNK):
        rows = slice(r, r + OUT_PROJ_ROW_CHUNK)
        y = jnp.dot(a_ref[rows, :], w_ref[...], preferred_element_type=F32)
        z = ALPHA * x_ref[rows, :] + m_ref[2:3, :] * y
        o_ref[rows, :] = _layer_norm_rows(z, g_ref[...], b_ref[...])


def _out_proj_sublayer(attn, x, mods, layer, cond_len, w_o, g, b):
    n_tok = x.shape[0]
    tok = lambda i: (i, 0)
    const = lambda i: (0, 0)
    return pl.pallas_call(
        _out_proj_kernel,
        grid=(n_tok // TOKEN_TILE,),
        in_specs=[
            pl.BlockSpec((TOKEN_TILE, D_MODEL), tok),
            pl.BlockSpec((TOKEN_TILE, D_MODEL), tok),
            _mod_spec(layer, 1, cond_len),
            _resident((D_MODEL, D_MODEL), const),
            pl.BlockSpec((1, D_MODEL), const),
            pl.BlockSpec((1, D_MODEL), const),
        ],
        out_specs=pl.BlockSpec((TOKEN_TILE, D_MODEL), tok),
        out_shape=jax.ShapeDtypeStruct((n_tok, D_MODEL), F32),
        compiler_params=_compiler_params(("arbitrary",)),
        name="out_proj",
    )(attn, x, mods, w_o, g.reshape(1, D_MODEL), b.reshape(1, D_MODEL))


def _pool_kernel(x_ref, prev_ref, next_ref, m_ref, w_ref, ps_ref, g_ref, b_ref, o_ref, h_ref,
                 *, tile, tiles_per_seq):
    i = pl.program_id(0)
    t_in_seq = i % tiles_per_seq
    scale1 = 1.0 + m_ref[1:2, :]
    shift = m_ref[0:1, :]
    x = x_ref[...]
    h_ref[POOL_HALO:POOL_HALO + tile, :] = x * scale1 + shift
    h_ref[0:POOL_HALO, :] = jnp.where(t_in_seq > 0, prev_ref[...] * scale1 + shift, 0.0)
    h_ref[POOL_HALO + tile:, :] = jnp.where(t_in_seq < tiles_per_seq - 1, next_ref[...] * scale1 + shift, 0.0)

    seq_len = tile * tiles_per_seq
    pos = t_in_seq * tile + jax.lax.broadcasted_iota(jnp.int32, (tile, 1), 0)
    ys = []
    for gi, win in enumerate(POOL_WINDOWS):
        cols = slice(gi * POOL_GC, (gi + 1) * POOL_GC)
        acc = None
        for k in range(-(win // 2), win // 2):
            term = h_ref[POOL_HALO + k:POOL_HALO + k + tile, cols]
            acc = term if acc is None else acc + term
        count = jnp.minimum(pos + win // 2, seq_len) - jnp.maximum(pos - win // 2, 0)
        d = acc / count.astype(F32) - h_ref[POOL_HALO:POOL_HALO + tile, cols]
        ys.append(jnp.dot(d.astype(BF16), w_ref[gi], preferred_element_type=F32))
    y = jnp.concatenate(ys, axis=-1) * ps_ref[...]
    z = ALPHA * x + m_ref[2:3, :] * y
    o_ref[...] = _layer_norm_rows(z, g_ref[...], b_ref[...])


def _pool_sublayer(x, mods, layer, cond_len, seq_len, w_pool, pool_scale, g, b):
    n_tok = x.shape[0]
    tile = min(TOKEN_TILE, seq_len)
    tiles_per_seq = seq_len // tile
    halo_blocks = tile // POOL_HALO
    n_halo_blocks = n_tok // POOL_HALO
    tok = lambda i: (i, 0)
    const = lambda i: (0, 0)
    return pl.pallas_call(
        functools.partial(_pool_kernel, tile=tile, tiles_per_seq=tiles_per_seq),
        grid=(n_tok // tile,),
        in_specs=[
            pl.BlockSpec((tile, D_MODEL), tok),
            pl.BlockSpec((POOL_HALO, D_MODEL), lambda i: (jnp.maximum(i * halo_blocks - 1, 0), 0)),
            pl.BlockSpec((POOL_HALO, D_MODEL),
                         lambda i: (jnp.minimum((i + 1) * halo_blocks, n_halo_blocks - 1), 0)),
            _mod_spec(layer, 1, cond_len, tile),
            _resident((len(POOL_WINDOWS), POOL_GC, POOL_GC), lambda i: (0, 0, 0)),
            pl.BlockSpec((1, D_MODEL), const),
            pl.BlockSpec((1, D_MODEL), const),
            pl.BlockSpec((1, D_MODEL), const),
        ],
        out_specs=pl.BlockSpec((tile, D_MODEL), tok),
        out_shape=jax.ShapeDtypeStruct((n_tok, D_MODEL), F32),
        scratch_shapes=[pltpu.VMEM((tile + 2 * POOL_HALO, D_MODEL), F32)],
        compiler_params=_compiler_params(("arbitrary",)),
        name="pool",
    )(x, x, x, mods, w_pool, pool_scale.reshape(1, D_MODEL), g.reshape(1, D_MODEL), b.reshape(1, D_MODEL))


def kernel(x_prompt, x_sample, cache_k, cache_v, c, c_ctx, w_mod, b_mod, ln_g, ln_b,
           ffn_w1, ffn_w3, ffn_w2, w_qkv, q_gain, k_gain, w_o, w_pool, pool_scale):
    n_ctx, ctx_len, _ = x_prompt.shape
    n_lat, lat_len, _ = x_sample.shape
    past = cache_k.shape[2]

    cond = jnp.zeros((N_COND_ROWS, D_MODEL), F32).at[:n_lat].set(c).at[CTX_ROW].set(c_ctx)
    mods = _modulation_table(cond, w_mod, b_mod)

    w1, w3, w2 = ffn_w1.astype(BF16), ffn_w3.astype(BF16), ffn_w2.astype(BF16)
    wqkv, wo, wp = w_qkv.astype(BF16), w_o.astype(BF16), w_pool.astype(BF16)
    rope = _rope_tables(lat_len)

    def backbone(x, cond_len, seq_len, latent):
        new_k, new_v = [], []
        for i in range(DEPTH):
            x = _ffn_sublayer(x, mods, i, 0, cond_len, w1[i, 0], w3[i, 0], w2[i, 0], ln_g[i, 0], ln_b[i, 0])
            j = i // 2
            if i % 2 == 0:
                if latent:
                    q, k, v = _qkv_heads(x, mods, i, cond_len, seq_len, wqkv[j], q_gain[j], k_gain[j], rope)
                    ck = cache_k[:, j].reshape(n_lat, past, D_KV)
                    cv = cache_v[:, j].reshape(n_lat, past, D_KV)
                    attn = _latent_attention(q, k, v, ck, cv, n_lat, seq_len)
                else:
                    q, k, v = _qkv_heads(x, mods, i, cond_len, seq_len, wqkv[j], q_gain[j], k_gain[j], None)
                    attn = _context_attention(q, k, v, n_ctx, seq_len)
                    new_k.append(k.reshape(n_ctx, seq_len, N_KV_HEADS, HEAD_DIM))
                    new_v.append(v.reshape(n_ctx, seq_len, N_KV_HEADS, HEAD_DIM))
                x = _out_proj_sublayer(attn, x, mods, i, cond_len, wo[j], ln_g[i, 1], ln_b[i, 1])
            else:
                x = _pool_sublayer(x, mods, i, cond_len, seq_len, wp[j], pool_scale[j], ln_g[i, 1], ln_b[i, 1])
            x = _ffn_sublayer(x, mods, i, 2, cond_len, w1[i, 1], w3[i, 1], w2[i, 1], ln_g[i, 2], ln_b[i, 2])
        return x, new_k, new_v

    y_ctx, ctx_k, ctx_v = backbone(x_prompt.reshape(n_ctx * ctx_len, D_MODEL), None, ctx_len, False)
    y_lat, _, _ = backbone(x_sample.reshape(n_lat * lat_len, D_MODEL), lat_len, lat_len, True)
    return (y_ctx.reshape(x_prompt.shape), y_lat.reshape(x_sample.shape),
            jnp.stack(ctx_k, axis=1), jnp.stack(ctx_v, axis=1))
```
